```python
import math
import jax, jax.numpy as jnp
from jax import lax
import numpy as np

D_MODEL = 1024
BATCH = 2
SEQ = 8192
DEPTH = 1
DEC_BATCH = 128
DEC_SEQ = 4
PAST_LEN = 16384
PAGE_SIZE = 128

MIX_WIDTH = D_MODEL
HEAD_DIM = 64
N_HEADS_MLA = (MIX_WIDTH // 2) // HEAD_DIM
N_HEADS_SB = (MIX_WIDTH // 2) // HEAD_DIM
MLA_NOPE_DIM = HEAD_DIM
MLA_ROPE_DIM = HEAD_DIM // 2
MLA_V_DIM = HEAD_DIM
Q_LORA = 3 * D_MODEL // 8
KV_LORA = D_MODEL // 4
ROPE_THETA = 10000.0
SB_COL0 = Q_LORA + KV_LORA + MLA_ROPE_DIM
IN_COLS = SB_COL0 + 3 * N_HEADS_SB * HEAD_DIM
N_GROUPS = 4
EXPERTS_PER_GROUP = 8
TOP_K_IN_GROUP = 2
D_FF_EXPERT = D_MODEL // 4
Q_BLOCK = 128
EPS = 1e-6
NEG_INF = -1e30
MLA_SCALE = 1.0 / math.sqrt(MLA_NOPE_DIM + MLA_ROPE_DIM)
SB_SCALE = 1.0 / math.sqrt(HEAD_DIM)

kernel_name = 'hymba_mla_stickbreak_hmoe_step'


def rmsnorm(x, g):
    xf = x.astype(jnp.float32)
    r = lax.rsqrt(jnp.mean(xf * xf, axis=-1, keepdims=True) + EPS)
    return (xf * r).astype(x.dtype) * g


def adaln(c, w_ada, b_ada):
    m = jax.nn.silu(c) @ w_ada + b_ada
    return jnp.split(m[:, None, :], 6, axis=-1)


def modulate(x, g, shift, scale):
    return rmsnorm(x, g) * (1.0 + scale) + shift


def rope_tables(pos):
    half = MLA_ROPE_DIM // 2
    inv_freq = ROPE_THETA ** (-jnp.arange(half, dtype=jnp.float32) / half)
    ang = pos.astype(jnp.float32)[:, None] * inv_freq[None, :]
    return jnp.cos(ang), jnp.sin(ang)


def apply_rope(x, cos, sin):
    x1, x2 = jnp.split(x, 2, axis=-1)
    return jnp.concatenate([x1 * cos - x2 * sin, x2 * cos + x1 * sin], axis=-1).astype(x.dtype)


def mixer_inputs(h, pos, w_in, q_norm_g, w_uq, kv_norm_g):
    B, T, _ = h.shape
    z = h @ w_in
    q_lat = z[..., :Q_LORA]
    c_kv = z[..., Q_LORA:Q_LORA + KV_LORA]
    k_r = z[..., Q_LORA + KV_LORA:SB_COL0]
    sb = z[..., SB_COL0:].reshape(B, T, 3, N_HEADS_SB, HEAD_DIM)
    q = (rmsnorm(q_lat, q_norm_g) @ w_uq).reshape(B, T, N_HEADS_MLA, MLA_NOPE_DIM + MLA_ROPE_DIM)
    cos, sin = rope_tables(pos)
    q_nope = q[..., :MLA_NOPE_DIM]
    q_rope = apply_rope(q[..., MLA_NOPE_DIM:], cos[None, :, None], sin[None, :, None])
    k_rope = apply_rope(k_r, cos[None], sin[None])
    ckv = rmsnorm(c_kv, kv_norm_g)
    return q_nope, q_rope, ckv, k_rope, sb[:, :, 0], sb[:, :, 1], sb[:, :, 2]


def to_blocks(a):
    B, S = a.shape[:2]
    return jnp.moveaxis(a.reshape((B, S // Q_BLOCK, Q_BLOCK) + a.shape[2:]), 1, 0)


def from_blocks(a):
    nb, B, qb = a.shape[:3]
    return jnp.moveaxis(a, 0, 1).reshape((B, nb * qb) + a.shape[3:])


def stick_breaking_weights(z, mask):
    log_beta = jax.nn.log_sigmoid(z)
    log_1mb = jnp.where(mask, jax.nn.log_sigmoid(-z), 0.0)
    acc = lax.cumsum(log_1mb, axis=z.ndim - 1, reverse=True) - log_1mb
    return jnp.where(mask, jnp.exp(log_beta + acc), 0.0)


def mla_prompt_attention(q_nope, q_rope, ckv, k_rope, pos, w_uk, w_uv):
    k_nope = jnp.einsum('bsc,chd->bshd', ckv, w_uk)
    v = jnp.einsum('bsc,chd->bshd', ckv, w_uv)

    def block(args):
        qn, qr, qp = args
        s = jnp.einsum('bqhd,bshd->bhqs', qn, k_nope) + jnp.einsum('bqhr,bsr->bhqs', qr, k_rope)
        s = jnp.where(pos[None, :] <= qp[:, None], s.astype(jnp.float32) * MLA_SCALE, NEG_INF)
        p = jax.nn.softmax(s, axis=-1).astype(v.dtype)
        return jnp.einsum('bhqs,bshd->bqhd', p, v)

    out = lax.map(block, (to_blocks(q_nope), to_blocks(q_rope), pos.reshape(-1, Q_BLOCK)))
    return from_blocks(out)


def sb_prompt_attention(q, k, v, pos):
    def block(args):
        qb, qp = args
        z = jnp.einsum('bqhd,bshd->bhqs', qb, k).astype(jnp.float32) * SB_SCALE
        a = stick_breaking_weights(z, pos[None, :] < qp[:, None]).astype(v.dtype)
        return jnp.einsum('bhqs,bshd->bqhd', a, v)

    return from_blocks(lax.map(block, (to_blocks(q), pos.reshape(-1, Q_BLOCK))))


def paged_sample_attention(q_nope, q_rope, ckv_new, krope_new, sb_q, sb_k, sb_v, pos, page_table,
                           cache_ckv, cache_krope, cache_k, cache_v, l, w_uk, w_uv):
    n_past = page_table.shape[1] * cache_ckv.shape[2]
    kpos = jnp.arange(n_past + pos.shape[0], dtype=jnp.int32)
    mask_mla = kpos[None, :] <= pos[:, None]
    mask_sb = kpos[None, :] < pos[:, None]
    q_abs = jnp.einsum('bqhd,chd->bqhc', q_nope, w_uk)

    def one_seq(args):
        pt, qa, qr, cn, kn, sq, sk, sv = args
        ckv = jnp.concatenate([cache_ckv[l, pt].reshape(n_past, KV_LORA), cn], axis=0)
        kr = jnp.concatenate([cache_krope[l, pt].reshape(n_past, MLA_ROPE_DIM), kn], axis=0)
        s = jnp.einsum('qhc,sc->hqs', qa, ckv) + jnp.einsum('qhr,sr->hqs', qr, kr)
        s = jnp.where(mask_mla, s.astype(jnp.float32) * MLA_SCALE, NEG_INF)
        p = jax.nn.softmax(s, axis=-1).astype(ckv.dtype)
        lat = jnp.einsum('hqs,sc->qhc', p, ckv)
        kb = jnp.concatenate([cache_k[l, pt].reshape(n_past, N_HEADS_SB, HEAD_DIM), sk], axis=0)
        vb = jnp.concatenate([cache_v[l, pt].reshape(n_past, N_HEADS_SB, HEAD_DIM), sv], axis=0)
        z = jnp.einsum('qhd,shd->hqs', sq, kb).astype(jnp.float32) * SB_SCALE
        a = stick_breaking_weights(z, mask_sb).astype(vb.dtype)
        return lat, jnp.einsum('hqs,shd->qhd', a, vb)

    lat, o_sb = lax.map(one_seq, (page_table, q_abs, q_rope, ckv_new, krope_new, sb_q, sb_k, sb_v))
    o_mla = jnp.einsum('bqhc,chd->bqhd', lat, w_uv)
    return o_mla, o_sb


def merge_heads(o_mla, o_sb, g_mla, g_sb, w_out):
    B, T = o_mla.shape[:2]
    o = jnp.concatenate([rmsnorm(o_mla.reshape(B, T, -1), g_mla),
                         rmsnorm(o_sb.reshape(B, T, -1), g_sb)], axis=-1)
    return o @ w_out


def hier_moe(h, w_rg, b_rg, w_re, b_re, w_gate, w_up, w_down):
    lg = (h @ w_rg + b_rg).astype(jnp.float32)
    g_sel = jnp.argmax(lg, axis=-1)
    p_sel = jnp.max(jax.nn.softmax(lg, axis=-1), axis=-1)
    g_oh = jax.nn.one_hot(g_sel, N_GROUPS, dtype=jnp.float32)
    le = (jnp.einsum('btd,gde->btge', h, w_re) + b_re).astype(jnp.float32)
    le_sel = jnp.einsum('btge,btg->bte', le, g_oh)
    top_v, top_i = lax.top_k(le_sel, TOP_K_IN_GROUP)
    w_top = jax.nn.softmax(top_v, axis=-1) * p_sel[..., None]
    ew = jnp.sum(jax.nn.one_hot(top_i, EXPERTS_PER_GROUP, dtype=jnp.float32) * w_top[..., None], axis=-2)
    out = jnp.zeros_like(h)
    for g in range(N_GROUPS):
        comb = (g_oh[..., g:g + 1] * ew).astype(h.dtype)
        act = jax.nn.silu(jnp.einsum('btd,edf->btef', h, w_gate[g])) * jnp.einsum('btd,edf->btef', h, w_up[g])
        out = out + jnp.einsum('btef,efd->btd', act * comb[..., None], w_down[g])
    return out


def setup_inputs(seed: int = 0) -> dict:
    key = jax.random.key(seed)
    ks = iter(list(jax.random.split(key, 32)))
    f32 = jnp.float32
    n_pages = PAST_LEN // PAGE_SIZE
    used = DEC_BATCH * n_pages
    n_pool = used + max(1, used // 4)
    hmla = N_HEADS_MLA * MLA_V_DIM
    hsb = N_HEADS_SB * HEAD_DIM

    def nrm(shape, scale=1.0):
        return jax.random.normal(next(ks), shape, f32) * scale

    def gain(shape):
        return 1.0 + 0.05 * nrm(shape)

    return dict(
        x_prompt=nrm((BATCH, SEQ, D_MODEL)),
        x_sample=nrm((DEC_BATCH, DEC_SEQ, D_MODEL)),
        c_prompt=nrm((BATCH, D_MODEL)),
        c_sample=nrm((DEC_BATCH, D_MODEL)),
        cache_mla_ckv=nrm((DEPTH, n_pool, PAGE_SIZE, KV_LORA)),
        cache_mla_krope=nrm((DEPTH, n_pool, PAGE_SIZE, MLA_ROPE_DIM)),
        cache_sb_k=nrm((DEPTH, n_pool, PAGE_SIZE, N_HEADS_SB, HEAD_DIM)),
        cache_sb_v=nrm((DEPTH, n_pool, PAGE_SIZE, N_HEADS_SB, HEAD_DIM)),
        page_table=jax.random.permutation(next(ks), n_pool)[:used].reshape(DEC_BATCH, n_pages).astype(jnp.int32),
        norm1_g=gain((DEPTH, D_MODEL)),
        norm2_g=gain((DEPTH, D_MODEL)),
        w_ada=nrm((DEPTH, D_MODEL, 6 * D_MODEL), 0.5 * D_MODEL ** -0.5),
        b_ada=nrm((DEPTH, 6 * D_MODEL), 0.02),
        w_in=nrm((DEPTH, D_MODEL, IN_COLS), D_MODEL ** -0.5),
        q_norm_g=gain((DEPTH, Q_LORA)),
        w_uq=nrm((DEPTH, Q_LORA, N_HEADS_MLA * (MLA_NOPE_DIM + MLA_ROPE_DIM)), Q_LORA ** -0.5),
        kv_norm_g=gain((DEPTH, KV_LORA)),
        w_uk=nrm((DEPTH, KV_LORA, N_HEADS_MLA, MLA_NOPE_DIM), KV_LORA ** -0.5),
        w_uv=nrm((DEPTH, KV_LORA, N_HEADS_MLA, MLA_V_DIM), KV_LORA ** -0.5),
        out_norm_mla_g=gain((DEPTH, hmla)),
        out_norm_sb_g=gain((DEPTH, hsb)),
        w_out=nrm((DEPTH, MIX_WIDTH, D_MODEL), MIX_WIDTH ** -0.5),
        w_router_group=nrm((DEPTH, D_MODEL, N_GROUPS), D_MODEL ** -0.5),
        b_router_group=nrm((DEPTH, N_GROUPS), 0.01),
        w_router_expert=nrm((DEPTH, N_GROUPS, D_MODEL, EXPERTS_PER_GROUP), D_MODEL ** -0.5),
        b_router_expert=nrm((DEPTH, N_GROUPS, EXPERTS_PER_GROUP), 0.01),
        w_gate=nrm((DEPTH, N_GROUPS, EXPERTS_PER_GROUP, D_MODEL, D_FF_EXPERT), D_MODEL ** -0.5),
        w_up=nrm((DEPTH, N_GROUPS, EXPERTS_PER_GROUP, D_MODEL, D_FF_EXPERT), D_MODEL ** -0.5),
        w_down=nrm((DEPTH, N_GROUPS, EXPERTS_PER_GROUP, D_FF_EXPERT, D_MODEL), D_FF_EXPERT ** -0.5),
        final_norm_g=gain((D_MODEL,)),
    )


def reference(x_prompt, x_sample, c_prompt, c_sample, cache_mla_ckv, cache_mla_krope, cache_sb_k, cache_sb_v,
              page_table, norm1_g, norm2_g, w_ada, b_ada, w_in, q_norm_g, w_uq, kv_norm_g, w_uk, w_uv,
              out_norm_mla_g, out_norm_sb_g, w_out, w_router_group, b_router_group, w_router_expert,
              b_router_expert, w_gate, w_up, w_down, final_norm_g):
    n_past = page_table.shape[1] * cache_mla_ckv.shape[2]
    pos_p = jnp.arange(x_prompt.shape[1], dtype=jnp.int32)
    pos_s = n_past + jnp.arange(x_sample.shape[1], dtype=jnp.int32)
    xp, xs = x_prompt, x_sample
    ckv_p, kr_p, sbk_p, sbv_p = [], [], [], []
    ckv_s, kr_s, sbk_s, sbv_s = [], [], [], []
    for l in range(DEPTH):
        sh1p, sc1p, g1p, sh2p, sc2p, g2p = adaln(c_prompt, w_ada[l], b_ada[l])
        sh1s, sc1s, g1s, sh2s, sc2s, g2s = adaln(c_sample, w_ada[l], b_ada[l])

        hp = modulate(xp, norm1_g[l], sh1p, sc1p)
        qn, qr, ckv, kr, sq, sk, sv = mixer_inputs(hp, pos_p, w_in[l], q_norm_g[l], w_uq[l], kv_norm_g[l])
        o_mla = mla_prompt_attention(qn, qr, ckv, kr, pos_p, w_uk[l], w_uv[l])
        o_sb = sb_prompt_attention(sq, sk, sv, pos_p)
        xp = xp + g1p * merge_heads(o_mla, o_sb, out_norm_mla_g[l], out_norm_sb_g[l], w_out[l])
        xp = xp + g2p * hier_moe(modulate(xp, norm2_g[l], sh2p, sc2p), w_router_group[l], b_router_group[l],
                                 w_router_expert[l], b_router_expert[l], w_gate[l], w_up[l], w_down[l])
        ckv_p.append(ckv); kr_p.append(kr); sbk_p.append(sk); sbv_p.append(sv)

        hs = modulate(xs, norm1_g[l], sh1s, sc1s)
        qn, qr, ckv, kr, sq, sk, sv = mixer_inputs(hs, pos_s, w_in[l], q_norm_g[l], w_uq[l], kv_norm_g[l])
        o_mla, o_sb = paged_sample_attention(qn, qr, ckv, kr, sq, sk, sv, pos_s, page_table, cache_mla_ckv,
                                             cache_mla_krope, cache_sb_k, cache_sb_v, l, w_uk[l], w_uv[l])
        xs = xs + g1s * merge_heads(o_mla, o_sb, out_norm_mla_g[l], out_norm_sb_g[l], w_out[l])
        xs = xs + g2s * hier_moe(modulate(xs, norm2_g[l], sh2s, sc2s), w_router_group[l], b_router_group[l],
                                 w_router_expert[l], b_router_expert[l], w_gate[l], w_up[l], w_down[l])
        ckv_s.append(ckv); kr_s.append(kr); sbk_s.append(sk); sbv_s.append(sv)

    y_prompt = rmsnorm(xp, final_norm_g)
    y_sample = rmsnorm(xs, final_norm_g)
    return (y_prompt, y_sample,
            jnp.stack(ckv_p), jnp.stack(kr_p), jnp.stack(sbk_p), jnp.stack(sbv_p),
            jnp.stack(ckv_s), jnp.stack(kr_s), jnp.stack(sbk_s), jnp.stack(sbv_s))
```

```python
import functools
import math

import jax
import jax.numpy as jnp
import numpy as np
from jax import lax
from jax.experimental import pallas as pl
from jax.experimental.pallas import tpu as pltpu

F32 = jnp.float32
BF16 = jnp.bfloat16

LANES = 128
HEAD_DIM = 64
ROPE_DIM = 32
ROPE_THETA = 10000.0
EPS = 1e-6
NEG_INF = -1e30
MLA_SCALE = 1.0 / math.sqrt(HEAD_DIM + ROPE_DIM)
SB_SCALE = 1.0 / math.sqrt(HEAD_DIM)
VMEM_LIMIT = 56 * 1024 * 1024
CUM_BLOCK = 256
SB_QROWS = 16
NEW_ROWS = 8

_NT = (((1,), (1,)), ((), ()))


def _cparams(sem):
    return pltpu.CompilerParams(dimension_semantics=sem, vmem_limit_bytes=VMEM_LIMIT)


def _full(shape):
    n = len(shape)
    return pl.BlockSpec(shape, lambda *_: (0,) * n)


def _rms(x):
    return x * lax.rsqrt(jnp.mean(x * x, axis=-1, keepdims=True) + EPS)


def _silu(x):
    return x / (1.0 + jnp.exp(-x))


def _log_sigmoid_pair(z):
    lb = jnp.minimum(z, 0.0) - jnp.log(1.0 + jnp.exp(-jnp.abs(z)))
    return lb, lb - z


def _split_dot(x, u):
    hi = x.astype(BF16)
    lo = (x - hi.astype(F32)).astype(BF16)
    return (jnp.dot(hi, u, preferred_element_type=F32) + jnp.dot(lo, u, preferred_element_type=F32))


def _adaln_kernel(c_ref, w_ref, b_ref, o_ref):
    c = c_ref[...]
    s = _silu(c).astype(BF16)
    o_ref[...] = jnp.dot(s, w_ref[...].astype(BF16), preferred_element_type=F32) + b_ref[...]


def _adaln(c, w, b):
    r, d = c.shape
    n = w.shape[1]
    tn = d
    return pl.pallas_call(
        _adaln_kernel,
        grid=(n // tn,),
        in_specs=[_full((r, d)), pl.BlockSpec((d, tn), lambda j: (0, j)), pl.BlockSpec((1, tn), lambda j: (0, j))],
        out_specs=pl.BlockSpec((r, tn), lambda j: (0, j)),
        out_shape=jax.ShapeDtypeStruct((r, n), F32),
        compiler_params=_cparams(("arbitrary",)),
        name="adaln",
    )(c, w, b.reshape(1, n))


def _premix_kernel(x_ref, shift_ref, scale_ref, g1_ref, win_ref, qg_ref, wuqa_ref, wuqb_ref, kvg_ref, wuk_ref,
                   wuv_ref, ekr_ref, cq_ref, sq_ref, ck_ref, sk_ref,
                   ckv_ref, kr_ref, sbk_ref, sbv_ref, qm_ref, km_ref, vm_ref, sqb_ref, skb_ref, svb_ref,
                   *, q_lora, kv_lora, sb_w, n_heads):
    x = x_ref[...]
    h = _rms(x) * g1_ref[...] * (1.0 + scale_ref[...]) + shift_ref[...]
    z = jnp.dot(h.astype(BF16), win_ref[...], preferred_element_type=F32)
    o = 0
    q_lat = z[:, o:o + q_lora]; o += q_lora
    c_kv = z[:, o:o + kv_lora]; o += kv_lora
    sq = z[:, o:o + sb_w]; o += sb_w
    sk = z[:, o:o + sb_w]; o += sb_w
    sv = z[:, o:o + sb_w]; o += sb_w
    zt = z[:, o:o + LANES]

    sbk_ref[...] = sk
    sbv_ref[...] = sv
    sqb_ref[...] = (sq * SB_SCALE).astype(BF16)
    skb_ref[...] = sk.astype(BF16)
    svb_ref[...] = sv.astype(BF16)

    qn = (_rms(q_lat) * qg_ref[...]).astype(BF16)
    qa = jnp.dot(qn, wuqa_ref[...], preferred_element_type=F32)
    qb = jnp.dot(qn, wuqb_ref[...], preferred_element_type=F32)
    cq = cq_ref[...]
    sq_t = sq_ref[...]
    for hd in range(n_heads):
        sl = slice(hd * LANES, (hd + 1) * LANES)
        qm_ref[:, sl] = (qa[:, sl] * cq + qb[:, sl] * sq_t).astype(BF16)

    ckv = _rms(c_kv) * kvg_ref[...]
    ckv_ref[...] = ckv
    ckv_b = ckv.astype(BF16)
    kr = zt[:, 0:ROPE_DIM] * ck_ref[...] + zt[:, ROPE_DIM:2 * ROPE_DIM] * sk_ref[...]
    kr_ref[...] = kr
    km = (jnp.dot(ckv_b, wuk_ref[...], preferred_element_type=F32)
          + jnp.dot(kr.astype(BF16), ekr_ref[...], preferred_element_type=F32))
    km_ref[...] = km.astype(BF16)
    vm_ref[...] = jnp.dot(ckv_b, wuv_ref[...], preferred_element_type=F32).astype(BF16)


def _premix(x, shift, scale, mod_rows, w, tabs, tm):
    t, d = x.shape
    n_tiles = t // tm
    tiles_per_group = n_tiles // shift.shape[0]
    q_lora, kv_lora, sb_w, n_heads = w["q_lora"], w["kv_lora"], w["sb_w"], w["n_heads"]
    qw = n_heads * LANES
    row = lambda i: (i, 0)
    mod_spec = pl.BlockSpec((None, mod_rows, d), lambda i: (i // tiles_per_group, 0, 0))
    in_specs = [
        pl.BlockSpec((tm, d), row), mod_spec, mod_spec, _full((1, d)), _full(w["w_in"].shape), _full((1, q_lora)),
        _full(w["w_uq_a"].shape), _full(w["w_uq_b"].shape), _full((1, kv_lora)), _full(w["w_uk"].shape),
        _full(w["w_uv"].shape), _full(w["e_kr"].shape),
        pl.BlockSpec((tm, LANES), row), pl.BlockSpec((tm, LANES), row),
        pl.BlockSpec((tm, ROPE_DIM), row), pl.BlockSpec((tm, ROPE_DIM), row),
    ]
    out_dims = [(kv_lora, F32), (ROPE_DIM, F32), (sb_w, F32), (sb_w, F32), (qw, BF16), (qw, BF16), (sb_w, BF16),
                (sb_w, BF16), (sb_w, BF16), (sb_w, BF16)]
    return pl.pallas_call(
        functools.partial(_premix_kernel, q_lora=q_lora, kv_lora=kv_lora, sb_w=sb_w, n_heads=n_heads),
        grid=(n_tiles,),
        in_specs=in_specs,
        out_specs=[pl.BlockSpec((tm, c), row) for c, _ in out_dims],
        out_shape=[jax.ShapeDtypeStruct((t, c), dt) for c, dt in out_dims],
        compiler_params=_cparams(("arbitrary",)),
        name="premix",
    )(x, shift, scale, w["norm1_g"], w["w_in"], w["q_norm_g"], w["w_uq_a"], w["w_uq_b"], w["kv_norm_g"], w["w_uk"],
      w["w_uv"], w["e_kr"], tabs["cq"], tabs["sq"], tabs["ck"], tabs["sk"])


def _tri_steps(n_blocks, descending):
    qi, ki = [], []
    for i in range(n_blocks):
        ks = range(i, -1, -1) if descending else range(i + 1)
        for k in ks:
            qi.append(i)
            ki.append(k)
    return jnp.asarray(qi, jnp.int32), jnp.asarray(ki, jnp.int32)


def _mla_prompt_kernel(qi_ref, ki_ref, q_ref, k_ref, v_ref, o_ref, m_scr, l_scr, acc_scr, *, tq):
    step = pl.program_id(2)
    qi = qi_ref[step]
    ki = ki_ref[step]

    @pl.when(ki == 0)
    def _():
        m_scr[...] = jnp.full(m_scr.shape, NEG_INF, F32)
        l_scr[...] = jnp.zeros(l_scr.shape, F32)
        acc_scr[...] = jnp.zeros(acc_scr.shape, F32)

    row = qi * tq + lax.broadcasted_iota(jnp.int32, (tq, tq), 0)
    col = ki * tq + lax.broadcasted_iota(jnp.int32, (tq, tq), 1)
    mask = col <= row
    v = v_ref[...]
    for hd in range(2):
        sl = slice(hd * LANES, (hd + 1) * LANES)
        s = lax.dot_general(q_ref[:, sl], k_ref[:, sl], _NT, preferred_element_type=F32)
        s = jnp.where(mask, s, NEG_INF)
        m_prev = m_scr[hd]
        m_new = jnp.maximum(m_prev, jnp.max(s, axis=-1, keepdims=True))
        alpha = jnp.exp(m_prev - m_new)
        p = jnp.exp(s - m_new)
        l_scr[hd] = alpha * l_scr[hd] + jnp.sum(p, axis=-1, keepdims=True)
        acc_scr[hd] = alpha * acc_scr[hd] + jnp.dot(p.astype(BF16), v, preferred_element_type=F32)
        m_scr[hd] = m_new

    @pl.when(ki == qi)
    def _():
        lane = lax.broadcasted_iota(jnp.int32, (tq, LANES), 1)
        o_ref[...] = jnp.where(lane < HEAD_DIM, acc_scr[0] / l_scr[0], acc_scr[1] / l_scr[1])


def _mla_prompt(qm, km, vm, batch, seq, tq):
    t = qm.shape[0]
    nq = seq // tq
    n_pairs = vm.shape[1] // LANES
    qi, ki = _tri_steps(nq, descending=False)
    grid_spec = pltpu.PrefetchScalarGridSpec(
        num_scalar_prefetch=2,
        grid=(batch, n_pairs, qi.shape[0]),
        in_specs=[
            pl.BlockSpec((tq, 2 * LANES), lambda b, p, s, qi, ki: (b * nq + qi[s], p)),
            pl.BlockSpec((tq, 2 * LANES), lambda b, p, s, qi, ki: (b * nq + ki[s], p)),
            pl.BlockSpec((tq, LANES), lambda b, p, s, qi, ki: (b * nq + ki[s], p)),
        ],
        out_specs=pl.BlockSpec((tq, LANES), lambda b, p, s, qi, ki: (b * nq + qi[s], p)),
        scratch_shapes=[pltpu.VMEM((2, tq, 1), F32), pltpu.VMEM((2, tq, 1), F32), pltpu.VMEM((2, tq, LANES), F32)],
    )
    return pl.pallas_call(
        functools.partial(_mla_prompt_kernel, tq=tq),
        grid_spec=grid_spec,
        out_shape=jax.ShapeDtypeStruct((t, vm.shape[1]), F32),
        compiler_params=_cparams(("arbitrary", "arbitrary", "arbitrary")),
        name="mla_prompt",
    )(qi, ki, qm, km, vm)


def _sb_prompt_kernel(qi_ref, ki_ref, q_ref, k_ref, v_ref, u_ref, o_ref, carry_scr, acc_scr, *, tq):
    step = pl.program_id(2)
    qi = qi_ref[step]
    ki = ki_ref[step]

    @pl.when(ki == qi)
    def _():
        carry_scr[...] = jnp.zeros(carry_scr.shape, F32)
        acc_scr[...] = jnp.zeros(acc_scr.shape, F32)

    cb = min(CUM_BLOCK, tq)
    row = qi * tq + lax.broadcasted_iota(jnp.int32, (tq, cb), 0)
    col0 = ki * tq + lax.broadcasted_iota(jnp.int32, (tq, cb), 1)
    lane = lax.broadcasted_iota(jnp.int32, (tq, LANES), 1)
    u = u_ref[...]
    q = q_ref[...]
    k = k_ref[...]
    for hd in range(2):
        in_head = (lane >= hd * HEAD_DIM) & (lane < (hd + 1) * HEAD_DIM)
        qh = jnp.where(in_head, q, jnp.zeros_like(q))
        z = lax.dot_general(qh, k, _NT, preferred_element_type=F32)
        carry = carry_scr[hd]
        acc = acc_scr[hd]
        for c in range(tq // cb - 1, -1, -1):
            zc = z[:, c * cb:(c + 1) * cb]
            mask = (col0 + c * cb) < row
            lb, l1 = _log_sigmoid_pair(zc)
            l1 = jnp.where(mask, l1, 0.0)
            later = _split_dot(l1, u)
            a = jnp.where(mask, jnp.exp(lb + later + carry), 0.0)
            acc = acc + jnp.dot(a.astype(BF16), v_ref[c * cb:(c + 1) * cb, :], preferred_element_type=F32)
            carry = carry + jnp.sum(l1, axis=-1, keepdims=True)
        carry_scr[hd] = carry
        acc_scr[hd] = acc

    @pl.when(ki == 0)
    def _():
        o_ref[...] = jnp.where(lane < HEAD_DIM, acc_scr[0], acc_scr[1])


def _later_matrix(n):
    j = np.arange(n)[:, None]
    s = np.arange(n)[None, :]
    return jnp.asarray((j > s).astype(np.float32), BF16)


def _sb_prompt(q, k, v, batch, seq, tq):
    t, w = q.shape
    nq = seq // tq
    n_pairs = w // LANES
    cb = min(CUM_BLOCK, tq)
    qi, ki = _tri_steps(nq, descending=True)
    grid_spec = pltpu.PrefetchScalarGridSpec(
        num_scalar_prefetch=2,
        grid=(batch, n_pairs, qi.shape[0]),
        in_specs=[
            pl.BlockSpec((tq, LANES), lambda b, p, s, qi, ki: (b * nq + qi[s], p)),
            pl.BlockSpec((tq, LANES), lambda b, p, s, qi, ki: (b * nq + ki[s], p)),
            pl.BlockSpec((tq, LANES), lambda b, p, s, qi, ki: (b * nq + ki[s], p)),
            pl.BlockSpec((cb, cb), lambda b, p, s, qi, ki: (0, 0)),
        ],
        out_specs=pl.BlockSpec((tq, LANES), lambda b, p, s, qi, ki: (b * nq + qi[s], p)),
        scratch_shapes=[pltpu.VMEM((2, tq, 1), F32), pltpu.VMEM((2, tq, LANES), F32)],
    )
    return pl.pallas_call(
        functools.partial(_sb_prompt_kernel, tq=tq),
        grid_spec=grid_spec,
        out_shape=jax.ShapeDtypeStruct((t, w), F32),
        compiler_params=_cparams(("arbitrary", "arbitrary", "arbitrary")),
        name="sb_prompt",
    )(qi, ki, q, k, v, _later_matrix(cb))


def _sample_attn_kernel(pt_ref, qn_ref, wukt_ref, qr_ref, qsb_ref, cnew_ref, krnew_ref, knew_ref, vnew_ref, u_ref,
                        wuv_ref, hmask_ref, *rest, pages_per_step, n_new, n_heads, n_past):
    npg = pages_per_step
    ckv_pages = rest[0:npg]
    kr_pages = rest[npg:2 * npg]
    k_pages = rest[2 * npg:3 * npg]
    v_pages = rest[3 * npg:4 * npg]
    omla_ref, osb_ref = rest[4 * npg:4 * npg + 2]
    qa_scr, m_scr, l_scr, lat_scr, carry_scr, sbo_scr = rest[4 * npg + 2:]
    j = pl.program_id(1)
    rows = qa_scr.shape[0]
    page = u_ref.shape[0]
    hrows = SB_QROWS

    def mla_update(ckv_b, kr_b, mask):
        s = (lax.dot_general(qa_scr[...], ckv_b, _NT, preferred_element_type=F32)
             + lax.dot_general(qr_ref[...], kr_b, _NT, preferred_element_type=F32))
        if mask is not None:
            s = jnp.where(mask, s, NEG_INF)
        m_prev = m_scr[...]
        m_new = jnp.maximum(m_prev, jnp.max(s, axis=-1, keepdims=True))
        alpha = jnp.exp(m_prev - m_new)
        p = jnp.exp(s - m_new)
        l_scr[...] = alpha * l_scr[...] + jnp.sum(p, axis=-1, keepdims=True)
        lat_scr[...] = alpha * lat_scr[...] + jnp.dot(p.astype(BF16), ckv_b, preferred_element_type=F32)
        m_scr[...] = m_new

    def sb_update(k_of_head, v_of_head, mask):
        z = jnp.concatenate(
            [lax.dot_general(qsb_ref[hd], k_of_head(hd), _NT, preferred_element_type=F32) for hd in range(n_heads)],
            axis=0)
        lb, l1 = _log_sigmoid_pair(z)
        if mask is not None:
            l1 = jnp.where(mask, l1, 0.0)
        a = jnp.exp(lb + _split_dot(l1, u_ref[...]) + carry_scr[...])
        if mask is not None:
            a = jnp.where(mask, a, 0.0)
        a = a.astype(BF16)
        for hd in range(n_heads):
            sbo_scr[hd] += jnp.dot(a[hd * hrows:(hd + 1) * hrows], v_of_head(hd), preferred_element_type=F32)
        carry_scr[...] += jnp.sum(l1, axis=-1, keepdims=True)

    @pl.when(j == 0)
    def _():
        qa_scr[...] = jnp.dot(qn_ref[...], wukt_ref[...], preferred_element_type=F32).astype(BF16)
        m_scr[...] = jnp.full(m_scr.shape, NEG_INF, F32)
        l_scr[...] = jnp.zeros(l_scr.shape, F32)
        lat_scr[...] = jnp.zeros(lat_scr.shape, F32)
        carry_scr[...] = jnp.zeros(carry_scr.shape, F32)
        sbo_scr[...] = jnp.zeros(sbo_scr.shape, F32)
        pad = page - cnew_ref.shape[0]

        def padded(x):
            return jnp.concatenate([x, jnp.zeros((pad, x.shape[1]), x.dtype)], axis=0).astype(BF16)

        key = lax.broadcasted_iota(jnp.int32, (rows, page), 1)
        qry = lax.broadcasted_iota(jnp.int32, (rows, page), 0) // n_heads
        mla_update(padded(cnew_ref[...]), padded(krnew_ref[...]), (key <= qry) & (key < n_new))
        key = lax.broadcasted_iota(jnp.int32, (n_heads * hrows, page), 1)
        qry = lax.broadcasted_iota(jnp.int32, (n_heads * hrows, page), 0) % hrows
        sb_update(lambda hd: padded(knew_ref[:, hd, :]), lambda hd: padded(vnew_ref[:, hd, :]),
                  (key < qry) & (key < n_new))

    for p in range(npg):
        mla_update(ckv_pages[p][...].astype(BF16), kr_pages[p][...].astype(BF16), None)
        sb_update(lambda hd, p=p: k_pages[p][:, hd, :].astype(BF16),
                  lambda hd, p=p: v_pages[p][:, hd, :].astype(BF16), None)

    @pl.when(j == pl.num_programs(1) - 1)
    def _():
        lat = (lat_scr[...] / l_scr[...]).astype(BF16)
        full = jnp.dot(lat, wuv_ref[...], preferred_element_type=F32) * hmask_ref[...]
        omla_ref[...] = jnp.sum(full.reshape(rows // n_heads, n_heads, full.shape[1]), axis=1)
        osb_ref[...] = sbo_scr[...]


def _sample_attn(page_table, qn_bd, wukt, qr, qsb, c_new, kr_new, k_new, v_new, wuv, caches, layer, pages_per_step):
    cache_ckv, cache_kr, cache_k, cache_v = caches
    b, n_pages = page_table.shape
    page = cache_ckv.shape[2]
    n_heads, hd_dim = cache_k.shape[3], cache_k.shape[4]
    kv_lora = cache_ckv.shape[3]
    n_new = qn_bd.shape[1] // n_heads
    rows = qn_bd.shape[1]
    pad_new = lambda x: jnp.pad(x, ((0, 0), (0, NEW_ROWS - n_new)) + ((0, 0),) * (x.ndim - 2))
    c_new, kr_new, k_new, v_new = pad_new(c_new), pad_new(kr_new), pad_new(k_new), pad_new(v_new)
    npg = pages_per_step
    n_steps = n_pages // npg
    hw = n_heads * hd_dim
    hmask = jnp.asarray((np.arange(rows)[:, None] % n_heads) == (np.arange(hw)[None, :] // hd_dim), F32)

    def per_seq(shape):
        n = len(shape)
        return pl.BlockSpec((None,) + shape, lambda i, j, pt: (i,) + (0,) * n)

    def const(shape):
        n = len(shape)
        return pl.BlockSpec(shape, lambda i, j, pt: (0,) * n)

    def paged(shape, p):
        n = len(shape)
        return pl.BlockSpec((None, None) + shape,
                            lambda i, j, pt: (layer, pt[i * n_pages + n_pages - 1 - (j * npg + p)]) + (0,) * n)

    in_specs = [per_seq((rows, hw)), const(wukt.shape), per_seq((rows, ROPE_DIM)),
                per_seq((n_heads, SB_QROWS, hd_dim)),
                per_seq((NEW_ROWS, kv_lora)), per_seq((NEW_ROWS, ROPE_DIM)), per_seq((NEW_ROWS, n_heads, hd_dim)),
                per_seq((NEW_ROWS, n_heads, hd_dim)), const((page, page)), const(wuv.shape), const(hmask.shape)]
    in_specs += [paged((page, kv_lora), p) for p in range(npg)]
    in_specs += [paged((page, ROPE_DIM), p) for p in range(npg)]
    in_specs += [paged((page, n_heads, hd_dim), p) for p in range(npg)]
    in_specs += [paged((page, n_heads, hd_dim), p) for p in range(npg)]
    grid_spec = pltpu.PrefetchScalarGridSpec(
        num_scalar_prefetch=1,
        grid=(b, n_steps),
        in_specs=in_specs,
        out_specs=[per_seq((n_new, hw)), per_seq((n_heads, SB_QROWS, hd_dim))],
        scratch_shapes=[pltpu.VMEM((rows, kv_lora), BF16), pltpu.VMEM((rows, 1), F32), pltpu.VMEM((rows, 1), F32),
                        pltpu.VMEM((rows, kv_lora), F32), pltpu.VMEM((n_heads * SB_QROWS, 1), F32),
                        pltpu.VMEM((n_heads, SB_QROWS, hd_dim), F32)],
    )
    return pl.pallas_call(
        functools.partial(_sample_attn_kernel, pages_per_step=npg, n_new=n_new, n_heads=n_heads,
                          n_past=n_pages * page),
        grid_spec=grid_spec,
        out_shape=[jax.ShapeDtypeStruct((b, n_new, hw), F32),
                   jax.ShapeDtypeStruct((b, n_heads, SB_QROWS, hd_dim), F32)],
        compiler_params=_cparams(("arbitrary", "arbitrary")),
        name="sample_attn",
    )(page_table.reshape(-1), qn_bd, wukt, qr, qsb, c_new, kr_new, k_new, v_new, _later_matrix(page), wuv, hmask,
      *([cache_ckv] * npg), *([cache_kr] * npg), *([cache_k] * npg), *([cache_v] * npg))


def _merge_kernel(x_ref, om_ref, os_ref, gm_ref, gs_ref, wout_ref, gate1_ref, shift2_ref, scale2_ref, g2_ref,
                  wr_ref, br_ref, x1_ref, h2_ref, comb_ref, *, n_groups, n_experts):
    om = (_rms(om_ref[...]) * gm_ref[...]).astype(BF16)
    osb = (_rms(os_ref[...]) * gs_ref[...]).astype(BF16)
    half = om.shape[1]
    mix = (jnp.dot(om, wout_ref[0:half, :], preferred_element_type=F32)
           + jnp.dot(osb, wout_ref[half:2 * half, :], preferred_element_type=F32))
    x1 = x_ref[...] + gate1_ref[...] * mix
    x1_ref[...] = x1
    h2 = _rms(x1) * g2_ref[...] * (1.0 + scale2_ref[...]) + shift2_ref[...]
    h2b = h2.astype(BF16)
    h2_ref[...] = h2b

    logits = jnp.dot(h2b, wr_ref[...], preferred_element_type=F32) + br_ref[...]
    lane = lax.broadcasted_iota(jnp.int32, logits.shape, 1)
    big = jnp.int32(2 * LANES)

    def first_max(vals):
        m = jnp.max(vals, axis=-1, keepdims=True)
        idx = jnp.min(jnp.where(vals == m, lane, big), axis=-1, keepdims=True)
        return m, idx

    is_group = lane < n_groups
    gl = jnp.where(is_group, logits, NEG_INF)
    gmax, gidx = first_max(gl)
    p_sel = 1.0 / jnp.sum(jnp.where(is_group, jnp.exp(gl - gmax), 0.0), axis=-1, keepdims=True)
    e_lane = lane - n_groups
    in_group = (e_lane >= gidx * n_experts) & (e_lane < (gidx + 1) * n_experts)
    el = jnp.where(in_group, logits, NEG_INF)
    v1, i1 = first_max(el)
    v2, i2 = first_max(jnp.where(lane == i1, NEG_INF, el))
    e2 = jnp.exp(v2 - v1)
    w1 = p_sel / (1.0 + e2)
    w2 = p_sel * e2 / (1.0 + e2)
    comb_ref[...] = jnp.where(lane == i1, w1, jnp.where(lane == i2, w2, 0.0))


def _merge(x, o_mla, o_sb, gate1, shift2, scale2, mod_rows, w, tm):
    t, d = x.shape
    half = o_mla.shape[1]
    n_tiles = t // tm
    tiles_per_group = n_tiles // gate1.shape[0]
    row = lambda i: (i, 0)
    mod_spec = pl.BlockSpec((None, mod_rows, d), lambda i: (i // tiles_per_group, 0, 0))
    return pl.pallas_call(
        functools.partial(_merge_kernel, n_groups=w["n_groups"], n_experts=w["n_experts"]),
        grid=(n_tiles,),
        in_specs=[pl.BlockSpec((tm, d), row), pl.BlockSpec((tm, half), row), pl.BlockSpec((tm, half), row),
                  _full((1, half)), _full((1, half)), _full(w["w_out"].shape), mod_spec, mod_spec, mod_spec,
                  _full((1, d)), _full(w["w_router"].shape), _full((1, LANES))],
        out_specs=[pl.BlockSpec((tm, d), row), pl.BlockSpec((tm, d), row), pl.BlockSpec((tm, LANES), row)],
        out_shape=[jax.ShapeDtypeStruct((t, d), F32), jax.ShapeDtypeStruct((t, d), BF16),
                   jax.ShapeDtypeStruct((t, LANES), F32)],
        compiler_params=_cparams(("arbitrary",)),
        name="merge_router",
    )(x, o_mla, o_sb, w["out_norm_mla_g"], w["out_norm_sb_g"], w["w_out"], gate1, shift2, scale2, w["norm2_g"],
      w["w_router"], w["b_router"])


def _moe_kernel(h2_ref, comb_ref, wg_ref, wu_ref, wd_ref, x1_ref, gate2_ref, gf_ref, y_ref, acc_scr, *, n_groups,
                final_norm):
    e = pl.program_id(1)

    @pl.when(e == 0)
    def _():
        acc_scr[...] = jnp.zeros(acc_scr.shape, F32)

    h2 = h2_ref[...]
    comb = comb_ref[...]
    lane = lax.broadcasted_iota(jnp.int32, comb.shape, 1)
    ce = jnp.sum(jnp.where(lane == e + n_groups, comb, 0.0), axis=-1, keepdims=True)
    g = jnp.dot(h2, wg_ref[...], preferred_element_type=F32)
    u = jnp.dot(h2, wu_ref[...], preferred_element_type=F32)
    act = _silu(g) * u * ce
    acc_scr[...] += jnp.dot(act.astype(BF16), wd_ref[...], preferred_element_type=F32)

    @pl.when(e == pl.num_programs(1) - 1)
    def _():
        x2 = x1_ref[...] + gate2_ref[...] * acc_scr[...]
        y_ref[...] = _rms(x2) * gf_ref[...] if final_norm else x2


def _moe(h2, comb, x1, gate2, mod_rows, w, tm, final_norm):
    t, d = x1.shape
    n_tiles = t // tm
    tiles_per_group = n_tiles // gate2.shape[0]
    n_exp, _, f = w["w_gate"].shape
    row = lambda i, e: (i, 0)
    return pl.pallas_call(
        functools.partial(_moe_kernel, n_groups=w["n_groups"], final_norm=final_norm),
        grid=(n_tiles, n_exp),
        in_specs=[pl.BlockSpec((tm, d), row), pl.BlockSpec((tm, LANES), row),
                  pl.BlockSpec((None, d, f), lambda i, e: (e, 0, 0)), pl.BlockSpec((None, d, f), lambda i, e: (e, 0, 0)),
                  pl.BlockSpec((None, f, d), lambda i, e: (e, 0, 0)), pl.BlockSpec((tm, d), row),
                  pl.BlockSpec((None, mod_rows, d), lambda i, e: (i // tiles_per_group, 0, 0)),
                  pl.BlockSpec((1, d), lambda i, e: (0, 0))],
        out_specs=pl.BlockSpec((tm, d), row),
        out_shape=jax.ShapeDtypeStruct((t, d), F32),
        scratch_shapes=[pltpu.VMEM((tm, d), F32)],
        compiler_params=_cparams(("arbitrary", "arbitrary")),
        name="moe_final",
    )(h2, comb, w["w_gate"], w["w_up"], w["w_down"], x1, gate2, w["final_norm_g"])


def _layout_weights(l, norm1_g, norm2_g, w_in, q_norm_g, w_uq, kv_norm_g, w_uk, w_uv, out_norm_mla_g, out_norm_sb_g,
                    w_out, w_router_group, b_router_group, w_router_expert, b_router_expert, w_gate, w_up, w_down,
                    final_norm_g):
    d = w_in.shape[1]
    q_lora = q_norm_g.shape[1]
    kv_lora = kv_norm_g.shape[1]
    n_heads = w_uk.shape[2]
    sb_w = (w_in.shape[2] - q_lora - kv_lora - ROPE_DIM) // 3
    half = ROPE_DIM // 2

    def swap(wr):
        return jnp.concatenate([-wr[..., half:], wr[..., :half]], axis=-1)

    wi = w_in[l]
    o_kr = q_lora + kv_lora
    o_sb = o_kr + ROPE_DIM
    w_kr = wi[:, o_kr:o_sb]
    w_in_ext = jnp.concatenate(
        [wi[:, :o_kr], wi[:, o_sb:], w_kr, swap(w_kr), jnp.zeros((d, LANES - 2 * ROPE_DIM), F32)], axis=1)

    wq = w_uq[l].reshape(q_lora, n_heads, HEAD_DIM + ROPE_DIM)
    zq = jnp.zeros((q_lora, n_heads, LANES - HEAD_DIM - ROPE_DIM), F32)
    w_uq_a = jnp.concatenate([wq, zq], axis=-1).reshape(q_lora, n_heads * LANES)
    w_uq_b = jnp.concatenate([jnp.zeros((q_lora, n_heads, HEAD_DIM), F32), swap(wq[..., HEAD_DIM:]), zq],
                             axis=-1).reshape(q_lora, n_heads * LANES)
    w_uk_ext = jnp.concatenate([w_uk[l], jnp.zeros((kv_lora, n_heads, LANES - HEAD_DIM), F32)],
                               axis=-1).reshape(kv_lora, n_heads * LANES)
    e_kr = np.zeros((ROPE_DIM, n_heads, LANES), np.float32)
    for r in range(ROPE_DIM):
        e_kr[r, :, HEAD_DIM + r] = 1.0

    n_groups = w_router_group.shape[2]
    n_experts = w_router_expert.shape[3]
    n_logits = n_groups + n_groups * n_experts
    w_router = jnp.concatenate(
        [w_router_group[l], jnp.moveaxis(w_router_expert[l], 0, 1).reshape(d, n_groups * n_experts),
         jnp.zeros((d, LANES - n_logits), F32)], axis=1)
    b_router = jnp.concatenate([b_router_group[l], b_router_expert[l].reshape(-1), jnp.zeros((LANES - n_logits,), F32)])
    f = w_gate.shape[-1]
    return dict(
        q_lora=q_lora, kv_lora=kv_lora, sb_w=sb_w, n_heads=n_heads, n_groups=n_groups, n_experts=n_experts,
        norm1_g=norm1_g[l][None], norm2_g=norm2_g[l][None], q_norm_g=q_norm_g[l][None], kv_norm_g=kv_norm_g[l][None],
        out_norm_mla_g=out_norm_mla_g[l][None], out_norm_sb_g=out_norm_sb_g[l][None],
        final_norm_g=final_norm_g[None],
        w_in=w_in_ext.astype(BF16), w_uq_a=w_uq_a.astype(BF16), w_uq_b=w_uq_b.astype(BF16),
        w_uk=w_uk_ext.astype(BF16), w_uv=w_uv[l].reshape(kv_lora, n_heads * HEAD_DIM).astype(BF16),
        w_ukt=w_uk[l].reshape(kv_lora, n_heads * HEAD_DIM).T.astype(BF16),
        e_kr=jnp.asarray(e_kr.reshape(ROPE_DIM, n_heads * LANES), BF16),
        w_out=w_out[l].astype(BF16), w_router=w_router.astype(BF16), b_router=b_router[None],
        w_gate=w_gate[l].reshape(n_groups * n_experts, d, f).astype(BF16),
        w_up=w_up[l].reshape(n_groups * n_experts, d, f).astype(BF16),
        w_down=w_down[l].reshape(n_groups * n_experts, f, d).astype(BF16),
    )


def _rope_tables(pos):
    half = ROPE_DIM // 2
    inv_freq = ROPE_THETA ** (-jnp.arange(half, dtype=F32) / half)
    ang = pos.astype(F32)[:, None] * inv_freq[None, :]
    cos2 = jnp.tile(jnp.cos(ang), (1, 2))
    sin2 = jnp.tile(jnp.sin(ang), (1, 2))
    n = pos.shape[0]
    pad = jnp.zeros((n, LANES - HEAD_DIM - ROPE_DIM), F32)
    cq = jnp.concatenate([jnp.ones((n, HEAD_DIM), F32), cos2, pad], axis=1) * MLA_SCALE
    sq = jnp.concatenate([jnp.zeros((n, HEAD_DIM), F32), sin2, pad], axis=1) * MLA_SCALE
    return dict(cq=cq, sq=sq, ck=cos2, sk=sin2)


def _tile(n, want):
    t = min(n, want)
    while n % t:
        t //= 2
    return t


def kernel(x_prompt, x_sample, c_prompt, c_sample, cache_mla_ckv, cache_mla_krope, cache_sb_k, cache_sb_v, page_table, norm1_g, norm2_g, w_ada, b_ada, w_in, q_norm_g, w_uq, kv_norm_g, w_uk, w_uv, out_norm_mla_g, out_norm_sb_g, w_out, w_router_group, b_router_group, w_router_expert, b_router_expert, w_gate, w_up, w_down, final_norm_g):
    bp, seq, d = x_prompt.shape
    bs, n_new, _ = x_sample.shape
    depth = w_in.shape[0]
    n_pages = page_table.shape[1]
    page = cache_mla_ckv.shape[2]
    n_past = n_pages * page
    n_heads = w_uk.shape[2]
    tp = bp * seq
    ts = bs * n_new

    xp = x_prompt.reshape(tp, d)
    xs = x_sample.reshape(ts, d)
    tabs_p = _rope_tables(jnp.tile(jnp.arange(seq, dtype=jnp.int32), bp))
    tabs_s = _rope_tables(jnp.tile(n_past + jnp.arange(n_new, dtype=jnp.int32), bs))
    c_all = jnp.concatenate([c_prompt, c_sample], axis=0)
    c_rows = -(-c_all.shape[0] // 8) * 8
    c_all = jnp.pad(c_all, ((0, c_rows - c_all.shape[0]), (0, 0)))

    tm_p = _tile(tp, 512)
    tm_s = _tile(ts, 512)
    tq = _tile(seq, 512)
    tm_moe = _tile(tp, 1024)
    pages_per_step = _tile(n_pages, 8)
    outs = [[] for _ in range(8)]
    for l in range(depth):
        w = _layout_weights(l, norm1_g, norm2_g, w_in, q_norm_g, w_uq, kv_norm_g, w_uk, w_uv, out_norm_mla_g,
                            out_norm_sb_g, w_out, w_router_group, b_router_group, w_router_expert, b_router_expert,
                            w_gate, w_up, w_down, final_norm_g)
        mod = _adaln(c_all, w_ada[l], b_ada[l])
        mod_p = [m[:, None, :] for m in jnp.split(mod[:bp], 6, axis=-1)]
        mod_s = [jnp.repeat(m, n_new, axis=0).reshape(ts // tm_s, tm_s, d)
                 for m in jnp.split(mod[bp:bp + bs], 6, axis=-1)]

        ckv, kr, sbk, sbv, qm, km, vm, sqb, skb, svb = _premix(xp, mod_p[0], mod_p[1], 1, w, tabs_p, tm_p)
        o_mla = _mla_prompt(qm, km, vm, bp, seq, tq)
        o_sb = _sb_prompt(sqb, skb, svb, bp, seq, tq)
        x1, h2, comb = _merge(xp, o_mla, o_sb, mod_p[2], mod_p[3], mod_p[4], 1, w, tm_p)
        xp = _moe(h2, comb, x1, mod_p[5], 1, w, tm_moe, l == depth - 1)
        outs[0].append(ckv.reshape(bp, seq, -1))
        outs[1].append(kr.reshape(bp, seq, -1))
        outs[2].append(sbk.reshape(bp, seq, n_heads, HEAD_DIM))
        outs[3].append(sbv.reshape(bp, seq, n_heads, HEAD_DIM))

        ckv, kr, sbk, sbv, qm, _, _, sqb, _, _ = _premix(xs, mod_s[0], mod_s[1], tm_s, w, tabs_s, tm_s)
        qm4 = qm.reshape(bs, n_new, n_heads, LANES)
        eye = jnp.eye(n_heads, dtype=BF16)
        qn_bd = (qm4[..., None, :HEAD_DIM] * eye[None, None, :, :, None]).reshape(bs, n_new * n_heads, -1)
        qr = qm4[..., HEAD_DIM:HEAD_DIM + ROPE_DIM].reshape(bs, n_new * n_heads, ROPE_DIM)
        qsb = jnp.pad(jnp.moveaxis(sqb.reshape(bs, n_new, n_heads, HEAD_DIM), 1, 2),
                      ((0, 0), (0, 0), (0, SB_QROWS - n_new), (0, 0)))
        o_mla, o_sb = _sample_attn(
            page_table, qn_bd, w["w_ukt"], qr, qsb, ckv.reshape(bs, n_new, -1), kr.reshape(bs, n_new, -1),
            sbk.reshape(bs, n_new, n_heads, HEAD_DIM), sbv.reshape(bs, n_new, n_heads, HEAD_DIM), w["w_uv"],
            (cache_mla_ckv, cache_mla_krope, cache_sb_k, cache_sb_v), l, pages_per_step)
        o_sb = jnp.moveaxis(o_sb[:, :, :n_new, :], 1, 2).reshape(ts, -1)
        x1, h2, comb = _merge(xs, o_mla.reshape(ts, -1), o_sb, mod_s[2], mod_s[3], mod_s[4], tm_s, w, tm_s)
        xs = _moe(h2, comb, x1, mod_s[5], tm_s, w, tm_s, l == depth - 1)
        outs[4].append(ckv.reshape(bs, n_new, -1))
        outs[5].append(kr.reshape(bs, n_new, -1))
        outs[6].append(sbk.reshape(bs, n_new, n_heads, HEAD_DIM))
        outs[7].append(sbv.reshape(bs, n_new, n_heads, HEAD_DIM))

    return (xp.reshape(bp, seq, d), xs.reshape(bs, n_new, d)) + tuple(jnp.stack(o) for o in outs)
```

```python
import functools
import math

import jax
import jax.numpy as jnp
import numpy as np
from jax import lax
from jax.experimental import pallas as pl
from jax.experimental.pallas import tpu as pltpu

F32 = jnp.float32
BF16 = jnp.bfloat16

LANES = 128
HEAD_DIM = 64
ROPE_DIM = 32
ROPE_THETA = 10000.0
EPS = 1e-6
NEG_INF = -1e30
MLA_SCALE = 1.0 / math.sqrt(HEAD_DIM + ROPE_DIM)
SB_SCALE = 1.0 / math.sqrt(HEAD_DIM)
VMEM_LIMIT = 56 * 1024 * 1024
CUM_BLOCK = 256
SB_DEAD = -150.0

_NT = (((1,), (1,)), ((), ()))


def _cparams(sem):
    return pltpu.CompilerParams(dimension_semantics=sem, vmem_limit_bytes=VMEM_LIMIT)


def _full(shape):
    n = len(shape)
    return pl.BlockSpec(shape, lambda *_: (0,) * n)


def _rms(x):
    return x * lax.rsqrt(jnp.mean(x * x, axis=-1, keepdims=True) + EPS)


def _silu(x):
    return x / (1.0 + jnp.exp(-x))


def _log_sigmoid_pair(z):
    lb = jnp.minimum(z, 0.0) - jnp.log(1.0 + jnp.exp(-jnp.abs(z)))
    return lb, lb - z


def _split_dot(x, u):
    hi = x.astype(BF16)
    lo = (x - hi.astype(F32)).astype(BF16)
    return (jnp.dot(hi, u, preferred_element_type=F32) + jnp.dot(lo, u, preferred_element_type=F32))


def _adaln_kernel(c_ref, w_ref, b_ref, o_ref):
    c = c_ref[...]
    s = _silu(c).astype(BF16)
    o_ref[...] = jnp.dot(s, w_ref[...].astype(BF16), preferred_element_type=F32) + b_ref[...]


def _adaln(c, w, b):
    r, d = c.shape
    n = w.shape[1]
    tn = d
    return pl.pallas_call(
        _adaln_kernel,
        grid=(n // tn,),
        in_specs=[_full((r, d)), pl.BlockSpec((d, tn), lambda j: (0, j)), pl.BlockSpec((1, tn), lambda j: (0, j))],
        out_specs=pl.BlockSpec((r, tn), lambda j: (0, j)),
        out_shape=jax.ShapeDtypeStruct((r, n), F32),
        compiler_params=_cparams(("arbitrary",)),
        name="adaln",
    )(c, w, b.reshape(1, n))


def _premix_kernel(x_ref, shift_ref, scale_ref, g1_ref, win_ref, qg_ref, wuqa_ref, wuqb_ref, kvg_ref, wuk_ref,
                   wuv_ref, ekr_ref, cq_ref, sq_ref, ck_ref, sk_ref,
                   ckv_ref, kr_ref, sbk_ref, sbv_ref, qm_ref, km_ref, vm_ref, sqb_ref, skb_ref, svb_ref,
                   *, q_lora, kv_lora, sb_w, n_heads):
    x = x_ref[...]
    h = _rms(x) * g1_ref[...] * (1.0 + scale_ref[...]) + shift_ref[...]
    z = jnp.dot(h.astype(BF16), win_ref[...], preferred_element_type=F32)
    o = 0
    q_lat = z[:, o:o + q_lora]; o += q_lora
    c_kv = z[:, o:o + kv_lora]; o += kv_lora
    sq = z[:, o:o + sb_w]; o += sb_w
    sk = z[:, o:o + sb_w]; o += sb_w
    sv = z[:, o:o + sb_w]; o += sb_w
    zt = z[:, o:o + LANES]

    sbk_ref[...] = sk
    sbv_ref[...] = sv
    sqb_ref[...] = (sq * SB_SCALE).astype(BF16)
    skb_ref[...] = sk.astype(BF16)
    svb_ref[...] = sv.astype(BF16)

    qn = (_rms(q_lat) * qg_ref[...]).astype(BF16)
    qa = jnp.dot(qn, wuqa_ref[...], preferred_element_type=F32)
    qb = jnp.dot(qn, wuqb_ref[...], preferred_element_type=F32)
    cq = cq_ref[...]
    sq_t = sq_ref[...]
    for hd in range(n_heads):
        sl = slice(hd * LANES, (hd + 1) * LANES)
        qm_ref[:, sl] = (qa[:, sl] * cq + qb[:, sl] * sq_t).astype(BF16)

    ckv = _rms(c_kv) * kvg_ref[...]
    ckv_ref[...] = ckv
    ckv_b = ckv.astype(BF16)
    kr = zt[:, 0:ROPE_DIM] * ck_ref[...] + zt[:, ROPE_DIM:2 * ROPE_DIM] * sk_ref[...]
    kr_ref[...] = kr
    km = (jnp.dot(ckv_b, wuk_ref[...], preferred_element_type=F32)
          + jnp.dot(kr.astype(BF16), ekr_ref[...], preferred_element_type=F32))
    km_ref[...] = km.astype(BF16)
    vm_ref[...] = jnp.dot(ckv_b, wuv_ref[...], preferred_element_type=F32).astype(BF16)


def _premix(x, shift, scale, mod_rows, w, tabs, tm):
    t, d = x.shape
    n_tiles = t // tm
    tiles_per_group = n_tiles // shift.shape[0]
    q_lora, kv_lora, sb_w, n_heads = w["q_lora"], w["kv_lora"], w["sb_w"], w["n_heads"]
    qw = n_heads * LANES
    row = lambda i: (i, 0)
    mod_spec = pl.BlockSpec((None, mod_rows, d), lambda i: (i // tiles_per_group, 0, 0))
    in_specs = [
        pl.BlockSpec((tm, d), row), mod_spec, mod_spec, _full((1, d)), _full(w["w_in"].shape), _full((1, q_lora)),
        _full(w["w_uq_a"].shape), _full(w["w_uq_b"].shape), _full((1, kv_lora)), _full(w["w_uk"].shape),
        _full(w["w_uv"].shape), _full(w["e_kr"].shape),
        pl.BlockSpec((tm, LANES), row), pl.BlockSpec((tm, LANES), row),
        pl.BlockSpec((tm, ROPE_DIM), row), pl.BlockSpec((tm, ROPE_DIM), row),
    ]
    out_dims = [(kv_lora, F32), (ROPE_DIM, F32), (sb_w, F32), (sb_w, F32), (qw, BF16), (qw, BF16), (sb_w, BF16),
                (sb_w, BF16), (sb_w, BF16), (sb_w, BF16)]
    return pl.pallas_call(
        functools.partial(_premix_kernel, q_lora=q_lora, kv_lora=kv_lora, sb_w=sb_w, n_heads=n_heads),
        grid=(n_tiles,),
        in_specs=in_specs,
        out_specs=[pl.BlockSpec((tm, c), row) for c, _ in out_dims],
        out_shape=[jax.ShapeDtypeStruct((t, c), dt) for c, dt in out_dims],
        compiler_params=_cparams(("arbitrary",)),
        name="premix",
    )(x, shift, scale, w["norm1_g"], w["w_in"], w["q_norm_g"], w["w_uq_a"], w["w_uq_b"], w["kv_norm_g"], w["w_uk"],
      w["w_uv"], w["e_kr"], tabs["cq"], tabs["sq"], tabs["ck"], tabs["sk"])


def _tri_steps(n_blocks, descending):
    qi, ki = [], []
    for i in range(n_blocks):
        ks = range(i, -1, -1) if descending else range(i + 1)
        for k in ks:
            qi.append(i)
            ki.append(k)
    return jnp.asarray(qi, jnp.int32), jnp.asarray(ki, jnp.int32)


def _mla_prompt_kernel(qi_ref, ki_ref, q_ref, k_ref, v_ref, o_ref, m_scr, l_scr, acc_scr, *, tq):
    step = pl.program_id(2)
    qi = qi_ref[step]
    ki = ki_ref[step]

    @pl.when(ki == 0)
    def _():
        m_scr[...] = jnp.full(m_scr.shape, NEG_INF, F32)
        l_scr[...] = jnp.zeros(l_scr.shape, F32)
        acc_scr[...] = jnp.zeros(acc_scr.shape, F32)

    row = qi * tq + lax.broadcasted_iota(jnp.int32, (tq, tq), 0)
    col = ki * tq + lax.broadcasted_iota(jnp.int32, (tq, tq), 1)
    mask = col <= row
    v = v_ref[...]
    for hd in range(2):
        sl = slice(hd * LANES, (hd + 1) * LANES)
        s = lax.dot_general(q_ref[:, sl], k_ref[:, sl], _NT, preferred_element_type=F32)
        s = jnp.where(mask, s, NEG_INF)
        m_prev = m_scr[hd]
        m_new = jnp.maximum(m_prev, jnp.max(s, axis=-1, keepdims=True))
        alpha = jnp.exp(m_prev - m_new)
        p = jnp.exp(s - m_new)
        l_scr[hd] = alpha * l_scr[hd] + jnp.sum(p, axis=-1, keepdims=True)
        acc_scr[hd] = alpha * acc_scr[hd] + jnp.dot(p.astype(BF16), v, preferred_element_type=F32)
        m_scr[hd] = m_new

    @pl.when(ki == qi)
    def _():
        lane = lax.broadcasted_iota(jnp.int32, (tq, LANES), 1)
        o_ref[...] = jnp.where(lane < HEAD_DIM, acc_scr[0] / l_scr[0], acc_scr[1] / l_scr[1])


def _mla_prompt(qm, km, vm, batch, seq, tq):
    t = qm.shape[0]
    nq = seq // tq
    n_pairs = vm.shape[1] // LANES
    qi, ki = _tri_steps(nq, descending=False)
    grid_spec = pltpu.PrefetchScalarGridSpec(
        num_scalar_prefetch=2,
        grid=(batch, n_pairs, qi.shape[0]),
        in_specs=[
            pl.BlockSpec((tq, 2 * LANES), lambda b, p, s, qi, ki: (b * nq + qi[s], p)),
            pl.BlockSpec((tq, 2 * LANES), lambda b, p, s, qi, ki: (b * nq + ki[s], p)),
            pl.BlockSpec((tq, LANES), lambda b, p, s, qi, ki: (b * nq + ki[s], p)),
        ],
        out_specs=pl.BlockSpec((tq, LANES), lambda b, p, s, qi, ki: (b * nq + qi[s], p)),
        scratch_shapes=[pltpu.VMEM((2, tq, 1), F32), pltpu.VMEM((2, tq, 1), F32), pltpu.VMEM((2, tq, LANES), F32)],
    )
    return pl.pallas_call(
        functools.partial(_mla_prompt_kernel, tq=tq),
        grid_spec=grid_spec,
        out_shape=jax.ShapeDtypeStruct((t, vm.shape[1]), F32),
        compiler_params=_cparams(("arbitrary", "arbitrary", "arbitrary")),
        name="mla_prompt",
    )(qi, ki, qm, km, vm)


def _sb_prompt_kernel(qi_ref, ki_ref, q_ref, k_ref, v_ref, u_ref, o_ref, carry_scr, acc_scr, *, tq):
    step = pl.program_id(2)
    qi = qi_ref[step]
    ki = ki_ref[step]

    @pl.when(ki == qi)
    def _():
        carry_scr[...] = jnp.zeros(carry_scr.shape, F32)
        acc_scr[...] = jnp.zeros(acc_scr.shape, F32)

    cb = min(CUM_BLOCK, tq)
    row = qi * tq + lax.broadcasted_iota(jnp.int32, (tq, cb), 0)
    col0 = ki * tq + lax.broadcasted_iota(jnp.int32, (tq, cb), 1)
    lane = lax.broadcasted_iota(jnp.int32, (tq, LANES), 1)

    @pl.when(jnp.max(carry_scr[...]) > SB_DEAD)
    def _():
        u = u_ref[...]
        q = q_ref[...]
        k = k_ref[...]
        for hd in range(2):
            in_head = (lane >= hd * HEAD_DIM) & (lane < (hd + 1) * HEAD_DIM)
            qh = jnp.where(in_head, q, jnp.zeros_like(q))
            z = lax.dot_general(qh, k, _NT, preferred_element_type=F32)
            carry = carry_scr[hd]
            acc = acc_scr[hd]
            for c in range(tq // cb - 1, -1, -1):
                zc = z[:, c * cb:(c + 1) * cb]
                mask = (col0 + c * cb) < row
                lb, l1 = _log_sigmoid_pair(zc)
                l1 = jnp.where(mask, l1, 0.0)
                later = _split_dot(l1, u)
                a = jnp.where(mask, jnp.exp(lb + later + carry), 0.0)
                acc = acc + jnp.dot(a.astype(BF16), v_ref[c * cb:(c + 1) * cb, :], preferred_element_type=F32)
                carry = carry + jnp.sum(l1, axis=-1, keepdims=True)
            carry_scr[hd] = carry
            acc_scr[hd] = acc

    @pl.when(ki == 0)
    def _():
        o_ref[...] = jnp.where(lane < HEAD_DIM, acc_scr[0], acc_scr[1])


def _later_matrix(n):
    j = np.arange(n)[:, None]
    s = np.arange(n)[None, :]
    return jnp.asarray((j > s).astype(np.float32), BF16)


def _sb_prompt(q, k, v, batch, seq, tq):
    t, w = q.shape
    nq = seq // tq
    n_pairs = w // LANES
    cb = min(CUM_BLOCK, tq)
    qi, ki = _tri_steps(nq, descending=True)
    grid_spec = pltpu.PrefetchScalarGridSpec(
        num_scalar_prefetch=2,
        grid=(batch, n_pairs, qi.shape[0]),
        in_specs=[
            pl.BlockSpec((tq, LANES), lambda b, p, s, qi, ki: (b * nq + qi[s], p)),
            pl.BlockSpec((tq, LANES), lambda b, p, s, qi, ki: (b * nq + ki[s], p)),
            pl.BlockSpec((tq, LANES), lambda b, p, s, qi, ki: (b * nq + ki[s], p)),
            pl.BlockSpec((cb, cb), lambda b, p, s, qi, ki: (0, 0)),
        ],
        out_specs=pl.BlockSpec((tq, LANES), lambda b, p, s, qi, ki: (b * nq + qi[s], p)),
        scratch_shapes=[pltpu.VMEM((2, tq, 1), F32), pltpu.VMEM((2, tq, LANES), F32)],
    )
    return pl.pallas_call(
        functools.partial(_sb_prompt_kernel, tq=tq),
        grid_spec=grid_spec,
        out_shape=jax.ShapeDtypeStruct((t, w), F32),
        compiler_params=_cparams(("arbitrary", "arbitrary", "arbitrary")),
        name="sb_prompt",
    )(qi, ki, q, k, v, _later_matrix(cb))


def _sample_attn_kernel(pt_ref, qn_ref, wukt_ref, qr_ref, qsb_ref, cnew_ref, krnew_ref, knew_ref, vnew_ref, u_ref,
                        wuv_ref, hmask_ref, k_hbm, v_hbm, *rest, pages_per_step, n_new, n_heads, n_pages, layer):
    npg = pages_per_step
    ckv_pages = rest[0:npg]
    kr_pages = rest[npg:2 * npg]
    omla_ref, osb_ref = rest[2 * npg:2 * npg + 2]
    qa_scr, m_scr, l_scr, lat_scr, carry_scr, sbo_scr, k_buf, v_buf, sem = rest[2 * npg + 2:]
    i = pl.program_id(0)
    j = pl.program_id(1)
    rows = qa_scr.shape[0]
    page = u_ref.shape[0]

    def page_copies(p):
        idx = pt_ref[i * n_pages + n_pages - 1 - (j * npg + p)]
        return (pltpu.make_async_copy(k_hbm.at[layer, idx], k_buf.at[p], sem.at[0, p]),
                pltpu.make_async_copy(v_hbm.at[layer, idx], v_buf.at[p], sem.at[1, p]))

    def mla_update(ckv_refs, kr_refs, mask):
        qa = qa_scr[...]
        qr = qr_ref[...]
        ckv_b = [r[...].astype(BF16) for r in ckv_refs]
        s = [lax.dot_general(qa, c, _NT, preferred_element_type=F32)
             + jnp.dot(qr, r[...].astype(BF16), preferred_element_type=F32) for c, r in zip(ckv_b, kr_refs)]
        if mask is not None:
            s = [jnp.where(mask, x, NEG_INF) for x in s]
        m_prev = m_scr[...]
        m_new = jnp.maximum(m_prev, jnp.max(functools.reduce(jnp.maximum, s), axis=-1, keepdims=True))
        alpha = jnp.exp(m_prev - m_new)
        p = [jnp.exp(x - m_new) for x in s]
        l_scr[...] = alpha * l_scr[...] + jnp.sum(sum(p), axis=-1, keepdims=True)
        lat_scr[...] = alpha * lat_scr[...] + sum(
            jnp.dot(x.astype(BF16), c, preferred_element_type=F32) for x, c in zip(p, ckv_b))
        m_scr[...] = m_new

    def sb_update(k_refs, v_refs, mask):
        qsb = qsb_ref[...]
        u = u_ref[...]
        carry = carry_scr[...]
        full = None
        for kr_, vr_ in zip(k_refs, v_refs):
            z = jnp.dot(qsb, kr_[...].astype(BF16), preferred_element_type=F32)
            lb, l1 = _log_sigmoid_pair(z)
            if mask is not None:
                l1 = jnp.where(mask, l1, 0.0)
            a = jnp.exp(lb + _split_dot(l1, u) + carry)
            if mask is not None:
                a = jnp.where(mask, a, 0.0)
            o = lax.dot_general(a.astype(BF16), vr_[...].astype(BF16), _NT, preferred_element_type=F32)
            full = o if full is None else full + o
            carry = carry + jnp.sum(l1, axis=-1, keepdims=True)
        sbo_scr[...] += full * hmask_ref[...]
        carry_scr[...] = carry

    @pl.when(j == 0)
    def _():
        qa_scr[...] = jnp.dot(qn_ref[...], wukt_ref[...], preferred_element_type=F32).astype(BF16)
        m_scr[...] = jnp.full(m_scr.shape, NEG_INF, F32)
        l_scr[...] = jnp.zeros(l_scr.shape, F32)
        lat_scr[...] = jnp.zeros(lat_scr.shape, F32)
        carry_scr[...] = jnp.zeros(carry_scr.shape, F32)
        sbo_scr[...] = jnp.zeros(sbo_scr.shape, F32)
        key = lax.broadcasted_iota(jnp.int32, (rows, page), 1)
        qry = lax.broadcasted_iota(jnp.int32, (rows, page), 0) // n_heads
        mla_update([cnew_ref], [krnew_ref], (key <= qry) & (key < n_new))
        sb_update([knew_ref], [vnew_ref], (key < qry) & (key < n_new))

    sb_live = jnp.max(carry_scr[...]) > SB_DEAD

    @pl.when(sb_live)
    def _():
        for p in range(npg):
            for c in page_copies(p):
                c.start()

    mla_update(ckv_pages, kr_pages, None)

    @pl.when(sb_live)
    def _():
        for p in range(npg):
            for c in page_copies(p):
                c.wait()
        sb_update([k_buf.at[p] for p in range(npg)], [v_buf.at[p] for p in range(npg)], None)

    @pl.when(j == pl.num_programs(1) - 1)
    def _():
        lat = (lat_scr[...] / l_scr[...]).astype(BF16)
        full = jnp.dot(lat, wuv_ref[...], preferred_element_type=F32) * hmask_ref[...]
        omla_ref[...] = jnp.sum(full.reshape(rows // n_heads, n_heads, full.shape[1]), axis=1)
        osb_ref[...] = jnp.sum(sbo_scr[...].reshape(rows // n_heads, n_heads, full.shape[1]), axis=1)


def _sample_attn(page_table, qn_bd, wukt, qr, qsb, c_new, kr_new, k_new, v_new, wuv, caches, layer, pages_per_step):
    cache_ckv, cache_kr, cache_k, cache_v = caches
    b, n_pages = page_table.shape
    depth, n_pool, page, kv_lora = cache_ckv.shape
    n_heads, hd_dim = cache_k.shape[3], cache_k.shape[4]
    hw = n_heads * hd_dim
    n_new = qn_bd.shape[1] // n_heads
    rows = qn_bd.shape[1]
    cache_kr = jnp.swapaxes(cache_kr, 2, 3)
    cache_k = jnp.transpose(cache_k, (0, 1, 3, 4, 2)).reshape(depth, n_pool, hw, page)
    cache_v = jnp.transpose(cache_v, (0, 1, 3, 4, 2)).reshape(depth, n_pool, hw, page)
    c_new = jnp.pad(c_new, ((0, 0), (0, page - n_new), (0, 0)))
    kr_new = jnp.pad(jnp.swapaxes(kr_new, 1, 2), ((0, 0), (0, 0), (0, page - n_new)))
    k_new = jnp.pad(jnp.swapaxes(k_new, 1, 2), ((0, 0), (0, 0), (0, page - n_new)))
    v_new = jnp.pad(jnp.swapaxes(v_new, 1, 2), ((0, 0), (0, 0), (0, page - n_new)))
    npg = pages_per_step
    n_steps = n_pages // npg
    hmask = jnp.asarray((np.arange(rows)[:, None] % n_heads) == (np.arange(hw)[None, :] // hd_dim), F32)

    def per_seq(shape):
        n = len(shape)
        return pl.BlockSpec((None,) + shape, lambda i, j, pt: (i,) + (0,) * n)

    def const(shape):
        n = len(shape)
        return pl.BlockSpec(shape, lambda i, j, pt: (0,) * n)

    def paged(shape, p):
        n = len(shape)
        return pl.BlockSpec((None, None) + shape,
                            lambda i, j, pt: (layer, pt[i * n_pages + n_pages - 1 - (j * npg + p)]) + (0,) * n)

    in_specs = [per_seq((rows, hw)), const(wukt.shape), per_seq((rows, ROPE_DIM)), per_seq((rows, hw)),
                per_seq((page, kv_lora)), per_seq((ROPE_DIM, page)), per_seq((hw, page)), per_seq((hw, page)),
                const((page, page)), const(wuv.shape), const(hmask.shape),
                pl.BlockSpec(memory_space=pl.ANY), pl.BlockSpec(memory_space=pl.ANY)]
    in_specs += [paged((page, kv_lora), p) for p in range(npg)]
    in_specs += [paged((ROPE_DIM, page), p) for p in range(npg)]
    grid_spec = pltpu.PrefetchScalarGridSpec(
        num_scalar_prefetch=1,
        grid=(b, n_steps),
        in_specs=in_specs,
        out_specs=[per_seq((n_new, hw)), per_seq((n_new, hw))],
        scratch_shapes=[pltpu.VMEM((rows, kv_lora), BF16), pltpu.VMEM((rows, 1), F32), pltpu.VMEM((rows, 1), F32),
                        pltpu.VMEM((rows, kv_lora), F32), pltpu.VMEM((rows, 1), F32), pltpu.VMEM((rows, hw), F32),
                        pltpu.VMEM((npg, hw, page), F32), pltpu.VMEM((npg, hw, page), F32),
                        pltpu.SemaphoreType.DMA((2, npg))],
    )
    return pl.pallas_call(
        functools.partial(_sample_attn_kernel, pages_per_step=npg, n_new=n_new, n_heads=n_heads, n_pages=n_pages,
                          layer=layer),
        grid_spec=grid_spec,
        out_shape=[jax.ShapeDtypeStruct((b, n_new, hw), F32), jax.ShapeDtypeStruct((b, n_new, hw), F32)],
        compiler_params=_cparams(("arbitrary", "arbitrary")),
        name="sample_attn",
    )(page_table.reshape(-1), qn_bd, wukt, qr, qsb, c_new, kr_new, k_new, v_new, _later_matrix(page), wuv, hmask,
      cache_k, cache_v, *([cache_ckv] * npg), *([cache_kr] * npg))


def _merge_kernel(x_ref, om_ref, os_ref, gm_ref, gs_ref, wout_ref, gate1_ref, shift2_ref, scale2_ref, g2_ref,
                  wr_ref, br_ref, x1_ref, h2_ref, comb_ref, *, n_groups, n_experts):
    om = (_rms(om_ref[...]) * gm_ref[...]).astype(BF16)
    osb = (_rms(os_ref[...]) * gs_ref[...]).astype(BF16)
    half = om.shape[1]
    mix = (jnp.dot(om, wout_ref[0:half, :], preferred_element_type=F32)
           + jnp.dot(osb, wout_ref[half:2 * half, :], preferred_element_type=F32))
    x1 = x_ref[...] + gate1_ref[...] * mix
    x1_ref[...] = x1
    h2 = _rms(x1) * g2_ref[...] * (1.0 + scale2_ref[...]) + shift2_ref[...]
    h2b = h2.astype(BF16)
    h2_ref[...] = h2b

    logits = jnp.dot(h2b, wr_ref[...], preferred_element_type=F32) + br_ref[...]
    lane = lax.broadcasted_iota(jnp.int32, logits.shape, 1)
    big = jnp.int32(2 * LANES)

    def first_max(vals):
        m = jnp.max(vals, axis=-1, keepdims=True)
        idx = jnp.min(jnp.where(vals == m, lane, big), axis=-1, keepdims=True)
        return m, idx

    is_group = lane < n_groups
    gl = jnp.where(is_group, logits, NEG_INF)
    gmax, gidx = first_max(gl)
    p_sel = 1.0 / jnp.sum(jnp.where(is_group, jnp.exp(gl - gmax), 0.0), axis=-1, keepdims=True)
    e_lane = lane - n_groups
    in_group = (e_lane >= gidx * n_experts) & (e_lane < (gidx + 1) * n_experts)
    el = jnp.where(in_group, logits, NEG_INF)
    v1, i1 = first_max(el)
    v2, i2 = first_max(jnp.where(lane == i1, NEG_INF, el))
    e2 = jnp.exp(v2 - v1)
    w1 = p_sel / (1.0 + e2)
    w2 = p_sel * e2 / (1.0 + e2)
    comb_ref[...] = jnp.where(lane == i1, w1, jnp.where(lane == i2, w2, 0.0))


def _merge(x, o_mla, o_sb, gate1, shift2, scale2, mod_rows, w, tm):
    t, d = x.shape
    half = o_mla.shape[1]
    n_tiles = t // tm
    tiles_per_group = n_tiles // gate1.shape[0]
    row = lambda i: (i, 0)
    mod_spec = pl.BlockSpec((None, mod_rows, d), lambda i: (i // tiles_per_group, 0, 0))
    return pl.pallas_call(
        functools.partial(_merge_kernel, n_groups=w["n_groups"], n_experts=w["n_experts"]),
        grid=(n_tiles,),
        in_specs=[pl.BlockSpec((tm, d), row), pl.BlockSpec((tm, half), row), pl.BlockSpec((tm, half), row),
                  _full((1, half)), _full((1, half)), _full(w["w_out"].shape), mod_spec, mod_spec, mod_spec,
                  _full((1, d)), _full(w["w_router"].shape), _full((1, LANES))],
        out_specs=[pl.BlockSpec((tm, d), row), pl.BlockSpec((tm, d), row), pl.BlockSpec((tm, LANES), row)],
        out_shape=[jax.ShapeDtypeStruct((t, d), F32), jax.ShapeDtypeStruct((t, d), BF16),
                   jax.ShapeDtypeStruct((t, LANES), F32)],
        compiler_params=_cparams(("arbitrary",)),
        name="merge_router",
    )(x, o_mla, o_sb, w["out_norm_mla_g"], w["out_norm_sb_g"], w["w_out"], gate1, shift2, scale2, w["norm2_g"],
      w["w_router"], w["b_router"])


def _moe_kernel(h2_ref, comb_ref, wg_ref, wu_ref, wd_ref, x1_ref, gate2_ref, gf_ref, y_ref, acc_scr, *, n_groups,
                final_norm):
    e = pl.program_id(1)

    @pl.when(e == 0)
    def _():
        acc_scr[...] = jnp.zeros(acc_scr.shape, F32)

    h2 = h2_ref[...]
    comb = comb_ref[...]
    lane = lax.broadcasted_iota(jnp.int32, comb.shape, 1)
    ce = jnp.sum(jnp.where(lane == e + n_groups, comb, 0.0), axis=-1, keepdims=True)
    g = jnp.dot(h2, wg_ref[...], preferred_element_type=F32)
    u = jnp.dot(h2, wu_ref[...], preferred_element_type=F32)
    act = _silu(g) * u * ce
    acc_scr[...] += jnp.dot(act.astype(BF16), wd_ref[...], preferred_element_type=F32)

    @pl.when(e == pl.num_programs(1) - 1)
    def _():
        x2 = x1_ref[...] + gate2_ref[...] * acc_scr[...]
        y_ref[...] = _rms(x2) * gf_ref[...] if final_norm else x2


def _moe(h2, comb, x1, gate2, mod_rows, w, tm, final_norm):
    t, d = x1.shape
    n_tiles = t // tm
    tiles_per_group = n_tiles // gate2.shape[0]
    n_exp, _, f = w["w_gate"].shape
    row = lambda i, e: (i, 0)
    return pl.pallas_call(
        functools.partial(_moe_kernel, n_groups=w["n_groups"], final_norm=final_norm),
        grid=(n_tiles, n_exp),
        in_specs=[pl.BlockSpec((tm, d), row), pl.BlockSpec((tm, LANES), row),
                  pl.BlockSpec((None, d, f), lambda i, e: (e, 0, 0)), pl.BlockSpec((None, d, f), lambda i, e: (e, 0, 0)),
                  pl.BlockSpec((None, f, d), lambda i, e: (e, 0, 0)), pl.BlockSpec((tm, d), row),
                  pl.BlockSpec((None, mod_rows, d), lambda i, e: (i // tiles_per_group, 0, 0)),
                  pl.BlockSpec((1, d), lambda i, e: (0, 0))],
        out_specs=pl.BlockSpec((tm, d), row),
        out_shape=jax.ShapeDtypeStruct((t, d), F32),
        scratch_shapes=[pltpu.VMEM((tm, d), F32)],
        compiler_params=_cparams(("arbitrary", "arbitrary")),
        name="moe_final",
    )(h2, comb, w["w_gate"], w["w_up"], w["w_down"], x1, gate2, w["final_norm_g"])


def _layout_weights(l, norm1_g, norm2_g, w_in, q_norm_g, w_uq, kv_norm_g, w_uk, w_uv, out_norm_mla_g, out_norm_sb_g,
                    w_out, w_router_group, b_router_group, w_router_expert, b_router_expert, w_gate, w_up, w_down,
                    final_norm_g):
    d = w_in.shape[1]
    q_lora = q_norm_g.shape[1]
    kv_lora = kv_norm_g.shape[1]
    n_heads = w_uk.shape[2]
    sb_w = (w_in.shape[2] - q_lora - kv_lora - ROPE_DIM) // 3
    half = ROPE_DIM // 2

    def swap(wr):
        return jnp.concatenate([-wr[..., half:], wr[..., :half]], axis=-1)

    wi = w_in[l]
    o_kr = q_lora + kv_lora
    o_sb = o_kr + ROPE_DIM
    w_kr = wi[:, o_kr:o_sb]
    w_in_ext = jnp.concatenate(
        [wi[:, :o_kr], wi[:, o_sb:], w_kr, swap(w_kr), jnp.zeros((d, LANES - 2 * ROPE_DIM), F32)], axis=1)

    wq = w_uq[l].reshape(q_lora, n_heads, HEAD_DIM + ROPE_DIM)
    zq = jnp.zeros((q_lora, n_heads, LANES - HEAD_DIM - ROPE_DIM), F32)
    w_uq_a = jnp.concatenate([wq, zq], axis=-1).reshape(q_lora, n_heads * LANES)
    w_uq_b = jnp.concatenate([jnp.zeros((q_lora, n_heads, HEAD_DIM), F32), swap(wq[..., HEAD_DIM:]), zq],
                             axis=-1).reshape(q_lora, n_heads * LANES)
    w_uk_ext = jnp.concatenate([w_uk[l], jnp.zeros((kv_lora, n_heads, LANES - HEAD_DIM), F32)],
                               axis=-1).reshape(kv_lora, n_heads * LANES)
    e_kr = np.zeros((ROPE_DIM, n_heads, LANES), np.float32)
    for r in range(ROPE_DIM):
        e_kr[r, :, HEAD_DIM + r] = 1.0

    n_groups = w_router_group.shape[2]
    n_experts = w_router_expert.shape[3]
    n_logits = n_groups + n_groups * n_experts
    w_router = jnp.concatenate(
        [w_router_group[l], jnp.moveaxis(w_router_expert[l], 0, 1).reshape(d, n_groups * n_experts),
         jnp.zeros((d, LANES - n_logits), F32)], axis=1)
    b_router = jnp.concatenate([b_router_group[l], b_router_expert[l].reshape(-1), jnp.zeros((LANES - n_logits,), F32)])
    f = w_gate.shape[-1]
    return dict(
        q_lora=q_lora, kv_lora=kv_lora, sb_w=sb_w, n_heads=n_heads, n_groups=n_groups, n_experts=n_experts,
        norm1_g=norm1_g[l][None], norm2_g=norm2_g[l][None], q_norm_g=q_norm_g[l][None], kv_norm_g=kv_norm_g[l][None],
        out_norm_mla_g=out_norm_mla_g[l][None], out_norm_sb_g=out_norm_sb_g[l][None],
        final_norm_g=final_norm_g[None],
        w_in=w_in_ext.astype(BF16), w_uq_a=w_uq_a.astype(BF16), w_uq_b=w_uq_b.astype(BF16),
        w_uk=w_uk_ext.astype(BF16), w_uv=w_uv[l].reshape(kv_lora, n_heads * HEAD_DIM).astype(BF16),
        w_ukt=w_uk[l].reshape(kv_lora, n_heads * HEAD_DIM).T.astype(BF16),
        e_kr=jnp.asarray(e_kr.reshape(ROPE_DIM, n_heads * LANES), BF16),
        w_out=w_out[l].astype(BF16), w_router=w_router.astype(BF16), b_router=b_router[None],
        w_gate=w_gate[l].reshape(n_groups * n_experts, d, f).astype(BF16),
        w_up=w_up[l].reshape(n_groups * n_experts, d, f).astype(BF16),
        w_down=w_down[l].reshape(n_groups * n_experts, f, d).astype(BF16),
    )


def _rope_tables(pos):
    half = ROPE_DIM // 2
    inv_freq = ROPE_THETA ** (-jnp.arange(half, dtype=F32) / half)
    ang = pos.astype(F32)[:, None] * inv_freq[None, :]
    cos2 = jnp.tile(jnp.cos(ang), (1, 2))
    sin2 = jnp.tile(jnp.sin(ang), (1, 2))
    n = pos.shape[0]
    pad = jnp.zeros((n, LANES - HEAD_DIM - ROPE_DIM), F32)
    cq = jnp.concatenate([jnp.ones((n, HEAD_DIM), F32), cos2, pad], axis=1) * MLA_SCALE
    sq = jnp.concatenate([jnp.zeros((n, HEAD_DIM), F32), sin2, pad], axis=1) * MLA_SCALE
    return dict(cq=cq, sq=sq, ck=cos2, sk=sin2)


def _tile(n, want):
    t = min(n, want)
    while n % t:
        t //= 2
    return t


def kernel(x_prompt, x_sample, c_prompt, c_sample, cache_mla_ckv, cache_mla_krope, cache_sb_k, cache_sb_v, page_table, norm1_g, norm2_g, w_ada, b_ada, w_in, q_norm_g, w_uq, kv_norm_g, w_uk, w_uv, out_norm_mla_g, out_norm_sb_g, w_out, w_router_group, b_router_group, w_router_expert, b_router_expert, w_gate, w_up, w_down, final_norm_g):
    bp, seq, d = x_prompt.shape
    bs, n_new, _ = x_sample.shape
    depth = w_in.shape[0]
    n_pages = page_table.shape[1]
    page = cache_mla_ckv.shape[2]
    n_past = n_pages * page
    n_heads = w_uk.shape[2]
    tp = bp * seq
    ts = bs * n_new

    xp = x_prompt.reshape(tp, d)
    xs = x_sample.reshape(ts, d)
    tabs_p = _rope_tables(jnp.tile(jnp.arange(seq, dtype=jnp.int32), bp))
    tabs_s = _rope_tables(jnp.tile(n_past + jnp.arange(n_new, dtype=jnp.int32), bs))
    c_all = jnp.concatenate([c_prompt, c_sample], axis=0)
    c_rows = -(-c_all.shape[0] // 8) * 8
    c_all = jnp.pad(c_all, ((0, c_rows - c_all.shape[0]), (0, 0)))

    tm_p = _tile(tp, 512)
    tm_s = _tile(ts, 512)
    tq = _tile(seq, 512)
    tm_moe = _tile(tp, 1024)
    pages_per_step = _tile(n_pages, 16)
    outs = [[] for _ in range(8)]
    for l in range(depth):
        w = _layout_weights(l, norm1_g, norm2_g, w_in, q_norm_g, w_uq, kv_norm_g, w_uk, w_uv, out_norm_mla_g,
                            out_norm_sb_g, w_out, w_router_group, b_router_group, w_router_expert, b_router_expert,
                            w_gate, w_up, w_down, final_norm_g)
        mod = _adaln(c_all, w_ada[l], b_ada[l])
        mod_p = [m[:, None, :] for m in jnp.split(mod[:bp], 6, axis=-1)]
        mod_s = [jnp.repeat(m, n_new, axis=0).reshape(ts // tm_s, tm_s, d)
                 for m in jnp.split(mod[bp:bp + bs], 6, axis=-1)]

        ckv, kr, sbk, sbv, qm, km, vm, sqb, skb, svb = _premix(xp, mod_p[0], mod_p[1], 1, w, tabs_p, tm_p)
        o_mla = _mla_prompt(qm, km, vm, bp, seq, tq)
        o_sb = _sb_prompt(sqb, skb, svb, bp, seq, tq)
        x1, h2, comb = _merge(xp, o_mla, o_sb, mod_p[2], mod_p[3], mod_p[4], 1, w, tm_p)
        xp = _moe(h2, comb, x1, mod_p[5], 1, w, tm_moe, l == depth - 1)
        outs[0].append(ckv.reshape(bp, seq, -1))
        outs[1].append(kr.reshape(bp, seq, -1))
        outs[2].append(sbk.reshape(bp, seq, n_heads, HEAD_DIM))
        outs[3].append(sbv.reshape(bp, seq, n_heads, HEAD_DIM))

        ckv, kr, sbk, sbv, qm, _, _, sqb, _, _ = _premix(xs, mod_s[0], mod_s[1], tm_s, w, tabs_s, tm_s)
        qm4 = qm.reshape(bs, n_new, n_heads, LANES)
        eye = jnp.eye(n_heads, dtype=BF16)
        qn_bd = (qm4[..., None, :HEAD_DIM] * eye[None, None, :, :, None]).reshape(bs, n_new * n_heads, -1)
        qr = qm4[..., HEAD_DIM:HEAD_DIM + ROPE_DIM].reshape(bs, n_new * n_heads, ROPE_DIM)
        sq4 = sqb.reshape(bs, n_new, n_heads, HEAD_DIM)
        qsb_bd = (sq4[..., None, :] * eye[None, None, :, :, None]).reshape(bs, n_new * n_heads, -1)
        o_mla, o_sb = _sample_attn(
            page_table, qn_bd, w["w_ukt"], qr, qsb_bd, ckv.reshape(bs, n_new, -1), kr.reshape(bs, n_new, -1),
            sbk.reshape(bs, n_new, -1), sbv.reshape(bs, n_new, -1), w["w_uv"],
            (cache_mla_ckv, cache_mla_krope, cache_sb_k, cache_sb_v), l, pages_per_step)
        x1, h2, comb = _merge(xs, o_mla.reshape(ts, -1), o_sb.reshape(ts, -1), mod_s[2], mod_s[3], mod_s[4], tm_s,
                              w, tm_s)
        xs = _moe(h2, comb, x1, mod_s[5], tm_s, w, tm_s, l == depth - 1)
        outs[4].append(ckv.reshape(bs, n_new, -1))
        outs[5].append(kr.reshape(bs, n_new, -1))
        outs[6].append(sbk.reshape(bs, n_new, n_heads, HEAD_DIM))
        outs[7].append(sbv.reshape(bs, n_new, n_heads, HEAD_DIM))

    return (xp.reshape(bp, seq, d), xs.reshape(bs, n_new, d)) + tuple(jnp.stack(o) for o in outs)
```
